```python
import jax, jax.numpy as jnp
from jax import lax
import numpy as np

D_MODEL = 2048
BATCH = 2
SEQ = 4096
DEPTH = 2
DEC_BATCH = 128
DEC_SEQ = 4
PAST_LEN = 2048
PAGE_SIZE = 128

HEAD_DIM = 128
N_MIX_HEADS = D_MODEL // HEAD_DIM
H_M = N_MIX_HEADS // 2
H_S = N_MIX_HEADS - H_M
W_M = H_M * HEAD_DIM
W_S = H_S * HEAD_DIM
N_IN = 4 * W_M + 2 * H_M + 3 * W_S
D_FF = ((8 * D_MODEL // 3 + 127) // 128) * 128
P_DIM = 256
CHUNK = 64
Q_BLOCK = 128
ALPHA = (2 * DEPTH) ** 0.25
BETA = (8 * DEPTH) ** -0.25
LN_EPS = 1e-5

kernel_name = 'hybrid_mlstm_stickbreak_decoder_step'


def layer_norm(x, g, b):
    xf = x.astype(jnp.float32)
    mu = xf.mean(-1, keepdims=True)
    var = jnp.mean(jnp.square(xf - mu), -1, keepdims=True)
    return ((xf - mu) * lax.rsqrt(var + LN_EPS) * g + b).astype(x.dtype)


def head_norm(h, g):
    mu = h.mean(-1, keepdims=True)
    var = jnp.mean(jnp.square(h - mu), -1, keepdims=True)
    return (h - mu) * lax.rsqrt(var + LN_EPS) * g.reshape(h.shape[2:]).astype(jnp.float32)


def swiglu(x, w_gu, w_dn):
    a, b = jnp.split(x @ w_gu, 2, axis=-1)
    return (jax.nn.silu(a) * b) @ w_dn


def project(x, w_in, b_if):
    u = x @ w_in
    idx = list(np.cumsum([W_M, W_M, W_M, W_M, H_M, H_M, W_S, W_S]))
    qm, km, vm, om, ig, fg, qs, ks, vs = jnp.split(u, idx, axis=-1)
    B, T = x.shape[:2]
    def heads(t, h):
        return t.reshape((B, T, h, HEAD_DIM)).astype(jnp.float32)
    qm, km, vm, om = heads(qm, H_M), heads(km, H_M) * HEAD_DIM ** -0.5, heads(vm, H_M), heads(om, H_M)
    ig = (ig + b_if[:H_M]).astype(jnp.float32)
    lf = jax.nn.log_sigmoid((fg + b_if[H_M:]).astype(jnp.float32))
    qs, ks, vs = heads(qs, H_S), heads(ks, H_S), heads(vs, H_S)
    return qm, km, vm, om, ig, lf, qs, ks, vs


def mlstm_chunk(state, q, k, v, ig, lf):
    C, n, m = state
    L = q.shape[2]
    b = jnp.cumsum(lf, axis=-1)
    causal = jnp.tril(jnp.ones((L, L), dtype=bool))
    dmat = jnp.where(causal, b[..., :, None] - b[..., None, :] + ig[..., None, :], -jnp.inf)
    inter = b + m[..., None]
    m_t = jnp.maximum(inter, dmat.max(-1))
    w_inter = jnp.exp(inter - m_t)
    s = jnp.einsum('bhtd,bhsd->bhts', q, k) * jnp.exp(dmat - m_t[..., None])
    num = jnp.einsum('bhts,bhsv->bhtv', s, v) + w_inter[..., None] * jnp.einsum('bhvd,bhtd->bhtv', C, q)
    den = s.sum(-1) + w_inter * jnp.einsum('bhd,bhtd->bht', n, q)
    h = num / jnp.maximum(jnp.abs(den), jnp.exp(-m_t))[..., None]
    m_new = m_t[..., -1]
    w_end = jnp.exp(b[..., -1:] - b + ig - m_new[..., None])
    w_old = jnp.exp(b[..., -1] + m - m_new)
    C_new = w_old[..., None, None] * C + jnp.einsum('bhs,bhsv,bhsd->bhvd', w_end, v, k)
    n_new = w_old[..., None] * n + jnp.einsum('bhs,bhsd->bhd', w_end, k)
    return (C_new, n_new, m_new), h


def mlstm_prompt(q, k, v, ig, lf):
    B, T, H, D = q.shape
    nc = T // CHUNK
    def to_chunks(t):
        t = t.reshape((B, nc, CHUNK) + t.shape[2:])
        return jnp.transpose(t, (1, 0, 3, 2) + tuple(range(4, t.ndim)))
    init = (jnp.zeros((B, H, D, D), jnp.float32), jnp.zeros((B, H, D), jnp.float32),
            jnp.zeros((B, H), jnp.float32))
    state, h = lax.scan(lambda st, xs: mlstm_chunk(st, *xs), init,
                        (to_chunks(q), to_chunks(k), to_chunks(v), to_chunks(ig), to_chunks(lf)))
    h = jnp.transpose(h, (1, 0, 3, 2, 4)).reshape((B, T, H, D))
    return state, h


def mlstm_sample(q, k, v, ig, lf, C, n, m):
    hm = lambda t: jnp.swapaxes(t, 1, 2)
    state = (C.astype(jnp.float32), n.astype(jnp.float32), m.astype(jnp.float32))
    state, h = mlstm_chunk(state, hm(q), hm(k), hm(v), hm(ig), hm(lf))
    return state, jnp.swapaxes(h, 1, 2)


def stick_breaking(q, k, v, q_pos, k_pos, bias):
    z = jnp.einsum('bqhd,bkhd->bhqk', q, k) * HEAD_DIM ** -0.5 + bias.astype(jnp.float32)[None, :, None, None]
    mask = k_pos[None, :] < q_pos[:, None]
    log_1m = jnp.where(mask, jax.nn.log_sigmoid(-z), 0.0)
    after = lax.cumsum(log_1m, axis=3, reverse=True) - log_1m
    a = jnp.where(mask, jnp.exp(jax.nn.log_sigmoid(z) + after), 0.0)
    return jnp.einsum('bhqk,bkhd->bqhd', a, v)


def sb_prompt(q, k, v, bias):
    B, T = q.shape[:2]
    k_pos = jnp.arange(T)
    def block(j):
        start = j * Q_BLOCK
        qb = lax.dynamic_slice_in_dim(q, start, Q_BLOCK, axis=1)
        return stick_breaking(qb, k, v, start + jnp.arange(Q_BLOCK), k_pos, bias)
    out = lax.map(block, jnp.arange(T // Q_BLOCK))
    return jnp.swapaxes(out, 0, 1).reshape(q.shape)


def sb_sample(q, k_new, v_new, cache_k, cache_v, page_table, bias):
    Bd, Tn = q.shape[:2]
    def gather(c):
        g = c[page_table]
        return g.reshape((Bd, -1) + c.shape[2:]).astype(jnp.float32)
    past = page_table.shape[1] * cache_k.shape[1]
    k_all = jnp.concatenate([gather(cache_k), k_new], axis=1)
    v_all = jnp.concatenate([gather(cache_v), v_new], axis=1)
    return stick_breaking(q, k_all, v_all, past + jnp.arange(Tn), jnp.arange(past + Tn), bias)


def decoder_layer(x, p, mixer, w1gu, w1dn, w_in, b_if, w_out, g_m, g_s, w2gu, w2dn,
                  w_pg, w_pp, ln_g, ln_b):
    B, T = x.shape[:2]
    x = layer_norm(ALPHA * x + 0.5 * swiglu(x, w1gu, w1dn), ln_g[0], ln_b[0])
    qm, km, vm, om, ig, lf, qs, ks, vs = project(x, w_in, b_if)
    hm, hs, new_state = mixer(qm, km, vm, ig, lf, qs, ks, vs)
    hm = head_norm(hm, g_m) * jax.nn.sigmoid(om)
    hs = head_norm(hs, g_s)
    mix = jnp.concatenate([hm.reshape((B, T, W_M)), hs.reshape((B, T, W_S))], -1).astype(x.dtype) @ w_out
    x = layer_norm(ALPHA * x + mix, ln_g[1], ln_b[1])
    x = layer_norm(ALPHA * x + 0.5 * swiglu(x, w2gu, w2dn), ln_g[2], ln_b[2])
    ple = jax.nn.sigmoid(x @ w_pg) * (p.astype(x.dtype) @ w_pp)
    x = layer_norm(ALPHA * x + ple, ln_g[3], ln_b[3])
    return x, new_state


def setup_inputs(seed: int = 0) -> dict:
    key = jax.random.key(seed)
    ks = jax.random.split(key, 28)
    nrm = lambda k, s: jax.random.normal(k, s, jnp.float32)
    n_pages = PAST_LEN // PAGE_SIZE
    used = DEC_BATCH * n_pages
    n_phys = used + max(1, used // 4)
    page_table = jax.random.permutation(ks[7], n_phys)[:used].reshape(DEC_BATCH, n_pages).astype(jnp.int32)
    b_if = jnp.concatenate([0.1 * nrm(ks[13], (DEPTH, H_M)),
                            jnp.linspace(3.0, 6.0, H_M)[None, :] + 0.1 * nrm(ks[14], (DEPTH, H_M))], axis=-1)
    b_sb = jnp.linspace(-9.0, -7.0, H_S)[None, :] + 0.1 * nrm(ks[24], (DEPTH, H_S))
    return {
        'x_prompt': nrm(ks[0], (BATCH, SEQ, D_MODEL)),
        'x_sample': nrm(ks[1], (DEC_BATCH, DEC_SEQ, D_MODEL)),
        'state_mlstm_C': nrm(ks[2], (DEPTH, DEC_BATCH, H_M, HEAD_DIM, HEAD_DIM)),
        'state_mlstm_n': nrm(ks[3], (DEPTH, DEC_BATCH, H_M, HEAD_DIM)),
        'state_mlstm_m': nrm(ks[4], (DEPTH, DEC_BATCH, H_M)),
        'cache_sb_k': nrm(ks[5], (DEPTH, n_phys, PAGE_SIZE, H_S, HEAD_DIM)),
        'cache_sb_v': nrm(ks[6], (DEPTH, n_phys, PAGE_SIZE, H_S, HEAD_DIM)),
        'page_table': page_table,
        'p_prompt': nrm(ks[8], (DEPTH, BATCH, SEQ, P_DIM)),
        'p_sample': nrm(ks[9], (DEPTH, DEC_BATCH, DEC_SEQ, P_DIM)),
        'w_ffn1_gu': nrm(ks[10], (DEPTH, D_MODEL, 2 * D_FF)) * D_MODEL ** -0.5,
        'w_ffn1_dn': nrm(ks[11], (DEPTH, D_FF, D_MODEL)) * (D_FF ** -0.5 * BETA),
        'w_in': nrm(ks[12], (DEPTH, D_MODEL, N_IN)) * D_MODEL ** -0.5,
        'b_if': b_if,
        'b_sb': b_sb,
        'w_out': nrm(ks[15], (DEPTH, D_MODEL, D_MODEL)) * (D_MODEL ** -0.5 * BETA),
        'g_head_m': 1.0 + 0.02 * nrm(ks[16], (DEPTH, W_M)),
        'g_head_s': 1.0 + 0.02 * nrm(ks[17], (DEPTH, W_S)),
        'w_ffn2_gu': nrm(ks[18], (DEPTH, D_MODEL, 2 * D_FF)) * D_MODEL ** -0.5,
        'w_ffn2_dn': nrm(ks[19], (DEPTH, D_FF, D_MODEL)) * (D_FF ** -0.5 * BETA),
        'w_ple_gate': nrm(ks[20], (DEPTH, D_MODEL, D_MODEL)) * D_MODEL ** -0.5,
        'w_ple_proj': nrm(ks[21], (DEPTH, P_DIM, D_MODEL)) * (P_DIM ** -0.5 * BETA),
        'ln_g': 1.0 + 0.02 * nrm(ks[22], (DEPTH, 4, D_MODEL)),
        'ln_b': 0.02 * nrm(ks[23], (DEPTH, 4, D_MODEL)),
    }


def reference(x_prompt, x_sample, state_mlstm_C, state_mlstm_n, state_mlstm_m, cache_sb_k, cache_sb_v,
              page_table, p_prompt, p_sample, w_ffn1_gu, w_ffn1_dn, w_in, b_if, b_sb, w_out, g_head_m,
              g_head_s, w_ffn2_gu, w_ffn2_dn, w_ple_gate, w_ple_proj, ln_g, ln_b):
    xp, xs = x_prompt, x_sample
    p_states, s_states = [], []
    for l in range(DEPTH):
        w_l = (w_ffn1_gu[l], w_ffn1_dn[l], w_in[l], b_if[l], w_out[l], g_head_m[l], g_head_s[l],
               w_ffn2_gu[l], w_ffn2_dn[l], w_ple_gate[l], w_ple_proj[l], ln_g[l], ln_b[l])

        def prompt_mixer(qm, km, vm, ig, lf, qs, ks, vs, l=l):
            st, hm = mlstm_prompt(qm, km, vm, ig, lf)
            return hm, sb_prompt(qs, ks, vs, b_sb[l]), st + (ks, vs)

        def sample_mixer(qm, km, vm, ig, lf, qs, ks, vs, l=l):
            st, hm = mlstm_sample(qm, km, vm, ig, lf, state_mlstm_C[l], state_mlstm_n[l], state_mlstm_m[l])
            hs = sb_sample(qs, ks, vs, cache_sb_k[l], cache_sb_v[l], page_table, b_sb[l])
            return hm, hs, st + (ks, vs)

        xp, stp = decoder_layer(xp, p_prompt[l], prompt_mixer, *w_l)
        xs, sts = decoder_layer(xs, p_sample[l], sample_mixer, *w_l)
        p_states.append(stp)
        s_states.append(sts)

    stack = lambda states, i: jnp.stack([s[i] for s in states], axis=0)
    return (xp, xs,
            stack(p_states, 0), stack(p_states, 1), stack(p_states, 2), stack(p_states, 3), stack(p_states, 4),
            stack(s_states, 0), stack(s_states, 1), stack(s_states, 2), stack(s_states, 3), stack(s_states, 4))
```

```python
import functools

import numpy as np
import jax
import jax.numpy as jnp
from jax import lax
from jax.experimental import pallas as pl
from jax.experimental.pallas import tpu as pltpu

F32 = jnp.float32
BF16 = jnp.bfloat16

HEAD_DIM = 128
LANES = 128
LN_EPS = 1e-5
MLSTM_CHUNK = 128
VMEM_LIMIT = 56 * 1024 * 1024


def _cparams(*sem):
    return pltpu.CompilerParams(dimension_semantics=sem, vmem_limit_bytes=VMEM_LIMIT)


def _dot(a, b):
    return jnp.dot(a, b, preferred_element_type=F32)


def _dot_nt(a, b):
    return lax.dot_general(a, b, (((1,), (1,)), ((), ())), preferred_element_type=F32)


def _dot_tn(a, b):
    return lax.dot_general(a, b, (((0,), (0,)), ((), ())), preferred_element_type=F32)


def _row_tile(m, target, mult=16):
    best = None
    for t in range(mult, min(m, target) + 1, mult):
        if m % t == 0:
            best = t
    assert best is not None, (m, target)
    return best


def _layer_norm(y, g, b):
    mu = jnp.mean(y, axis=-1, keepdims=True)
    yc = y - mu
    var = jnp.mean(yc * yc, axis=-1, keepdims=True)
    return yc * lax.rsqrt(var + LN_EPS) * g + b


def _head_norm(h, g):
    mu = jnp.mean(h, axis=-1, keepdims=True)
    hc = h - mu
    var = jnp.mean(hc * hc, axis=-1, keepdims=True)
    return hc * lax.rsqrt(var + LN_EPS) * g


def _softplus_neg_abs(z):
    return jnp.log1p(jnp.exp(-jnp.abs(z)))


def _sigmoid(x):
    return 1.0 / (1.0 + jnp.exp(-x))


def _ffn_up_kernel(x_ref, wg_ref, wu_ref, h_ref):
    x = x_ref[...]
    a = _dot(x, wg_ref[...])
    b = _dot(x, wu_ref[...])
    h_ref[...] = (a * _sigmoid(a) * b).astype(h_ref.dtype)


def ffn_up(xb, wgu, *, tm, tf):
    m, d = xb.shape
    fp = wgu.shape[2]
    return pl.pallas_call(
        _ffn_up_kernel,
        grid=(m // tm, fp // tf),
        in_specs=[pl.BlockSpec((tm, d), lambda i, j: (i, 0)),
                  pl.BlockSpec((None, d, tf), lambda i, j: (0, 0, j)),
                  pl.BlockSpec((None, d, tf), lambda i, j: (1, 0, j))],
        out_specs=pl.BlockSpec((tm, tf), lambda i, j: (i, j)),
        out_shape=jax.ShapeDtypeStruct((m, fp), BF16),
        compiler_params=_cparams("parallel", "arbitrary"),
        name="ffn_up",
    )(xb, wgu, wgu)


def _mm_res_ln_kernel(h_ref, w_ref, x_ref, g_ref, b_ref, y_ref, yb_ref, acc_ref, *, alpha, scale, nk):
    k = pl.program_id(1)
    part = _dot(h_ref[...], w_ref[...])

    @pl.when(k == 0)
    def _():
        acc_ref[...] = part

    @pl.when(k > 0)
    def _():
        acc_ref[...] += part

    @pl.when(k == nk - 1)
    def _():
        y = _layer_norm(alpha * x_ref[...] + scale * acc_ref[...], g_ref[...], b_ref[...])
        y_ref[...] = y
        yb_ref[...] = y.astype(BF16)


def mm_res_ln(h, w, x, g, b, *, alpha, scale, tm, tk):
    m, kdim = h.shape
    d = w.shape[1]
    nk = kdim // tk
    kern = functools.partial(_mm_res_ln_kernel, alpha=alpha, scale=scale, nk=nk)
    return pl.pallas_call(
        kern,
        grid=(m // tm, nk),
        in_specs=[pl.BlockSpec((tm, tk), lambda i, k: (i, k)),
                  pl.BlockSpec((tk, d), lambda i, k: (k, 0)),
                  pl.BlockSpec((tm, d), lambda i, k: (i, 0)),
                  pl.BlockSpec((1, d), lambda i, k: (0, 0)),
                  pl.BlockSpec((1, d), lambda i, k: (0, 0))],
        out_specs=[pl.BlockSpec((tm, d), lambda i, k: (i, 0)),
                   pl.BlockSpec((tm, d), lambda i, k: (i, 0))],
        out_shape=[jax.ShapeDtypeStruct((m, d), F32), jax.ShapeDtypeStruct((m, d), BF16)],
        scratch_shapes=[pltpu.VMEM((tm, d), F32)],
        compiler_params=_cparams("parallel", "arbitrary"),
        name="mm_res_ln",
    )(h, w, x, g, b)


def _ple_kernel(xb_ref, wg_ref, p_ref, wp_ref, x_ref, g_ref, b_ref, y_ref, yb_ref, acc_ref, *, alpha, nk):
    k = pl.program_id(1)
    part = _dot(xb_ref[...], wg_ref[...])

    @pl.when(k == 0)
    def _():
        acc_ref[...] = part

    @pl.when(k > 0)
    def _():
        acc_ref[...] += part

    @pl.when(k == nk - 1)
    def _():
        proj = _dot(p_ref[...].astype(BF16), wp_ref[...])
        y = _layer_norm(alpha * x_ref[...] + _sigmoid(acc_ref[...]) * proj, g_ref[...], b_ref[...])
        y_ref[...] = y
        yb_ref[...] = y.astype(BF16)


def ple(xb, wg, p, wp, x, g, b, *, alpha, tm, tk):
    m, d = xb.shape
    pd = p.shape[1]
    nk = d // tk
    kern = functools.partial(_ple_kernel, alpha=alpha, nk=nk)
    return pl.pallas_call(
        kern,
        grid=(m // tm, nk),
        in_specs=[pl.BlockSpec((tm, tk), lambda i, k: (i, k)),
                  pl.BlockSpec((tk, d), lambda i, k: (k, 0)),
                  pl.BlockSpec((tm, pd), lambda i, k: (i, 0)),
                  pl.BlockSpec((pd, d), lambda i, k: (0, 0)),
                  pl.BlockSpec((tm, d), lambda i, k: (i, 0)),
                  pl.BlockSpec((1, d), lambda i, k: (0, 0)),
                  pl.BlockSpec((1, d), lambda i, k: (0, 0))],
        out_specs=[pl.BlockSpec((tm, d), lambda i, k: (i, 0)),
                   pl.BlockSpec((tm, d), lambda i, k: (i, 0))],
        out_shape=[jax.ShapeDtypeStruct((m, d), F32), jax.ShapeDtypeStruct((m, d), BF16)],
        scratch_shapes=[pltpu.VMEM((tm, d), F32)],
        compiler_params=_cparams("parallel", "arbitrary"),
        name="ple",
    )(xb, wg, p, wp, x, g, b)


def _proj_kernel(x_ref, w_ref, qm_ref, km_ref, vm_ref, om_ref, qs_ref, ks_ref, ksb_ref, vs_ref, vsb_ref,
                 *, k_scale):
    j = pl.program_id(1)
    u = _dot(x_ref[...], w_ref[...])

    @pl.when(j == 0)
    def _():
        qm_ref[...] = u.astype(BF16)

    @pl.when(j == 1)
    def _():
        km_ref[...] = (u * k_scale).astype(BF16)

    @pl.when(j == 2)
    def _():
        vm_ref[...] = u.astype(BF16)

    @pl.when(j == 3)
    def _():
        om_ref[...] = u

    @pl.when(j == 4)
    def _():
        qs_ref[...] = u.astype(BF16)

    @pl.when(j == 5)
    def _():
        ks_ref[...] = u
        ksb_ref[...] = u.astype(BF16)

    @pl.when(j == 6)
    def _():
        vs_ref[...] = u
        vsb_ref[...] = u.astype(BF16)


def proj(xb, w, *, width, tm):
    m, d = xb.shape
    assert w.shape[1] == 7 * width
    dts = [BF16, BF16, BF16, F32, BF16, F32, BF16, F32, BF16]
    return pl.pallas_call(
        functools.partial(_proj_kernel, k_scale=HEAD_DIM ** -0.5),
        grid=(m // tm, 7),
        in_specs=[pl.BlockSpec((tm, d), lambda i, j: (i, 0)),
                  pl.BlockSpec((d, width), lambda i, j: (0, j))],
        out_specs=[pl.BlockSpec((tm, width), lambda i, j: (i, 0)) for _ in dts],
        out_shape=[jax.ShapeDtypeStruct((m, width), dt) for dt in dts],
        compiler_params=_cparams("parallel", "arbitrary"),
        name="proj",
    )(xb, w)


def _gates_kernel(x_ref, w_ref, bias_ref, ig_ref, lf_ref, bq_ref, cmq_ref, rq_ref, rt_ref, *, chunk, heads):
    u = _dot(x_ref[...], w_ref[...]) + bias_ref[...]
    ig = u[:, :LANES]
    fp = u[:, LANES:]
    lf = jnp.minimum(fp, 0.0) - _softplus_neg_abs(fp)
    ig_ref[...] = ig
    lf_ref[...] = lf
    tg = ig.shape[0]
    pos = lax.broadcasted_iota(jnp.int32, (tg, LANES), 0) & (chunk - 1)
    b = lf
    d = 1
    while d < chunk:
        b = b + jnp.where(pos >= d, pltpu.roll(b, d, axis=0), 0.0)
        d *= 2
    r = ig - b
    cm = r
    d = 1
    while d < chunk:
        cm = jnp.maximum(cm, jnp.where(pos >= d, pltpu.roll(cm, d, axis=0), -jnp.inf))
        d *= 2
    for h in range(heads):
        bq_ref[h] = jnp.broadcast_to(b[:, h:h + 1], (tg, LANES))
        cmq_ref[h] = jnp.broadcast_to(cm[:, h:h + 1], (tg, LANES))
        rq_ref[h] = jnp.broadcast_to(r[:, h:h + 1], (tg, LANES))
    rt_ref[...] = r.T


def gates(xb, w, bias, *, heads, chunk, tg):
    m, d = xb.shape
    assert tg % chunk == 0 and chunk & (chunk - 1) == 0
    tok = jax.ShapeDtypeStruct((m, LANES), F32)
    per_head = jax.ShapeDtypeStruct((heads, m, LANES), F32)
    return pl.pallas_call(
        functools.partial(_gates_kernel, chunk=chunk, heads=heads),
        grid=(m // tg,),
        in_specs=[pl.BlockSpec((tg, d), lambda i: (i, 0)),
                  pl.BlockSpec((d, 2 * LANES), lambda i: (0, 0)),
                  pl.BlockSpec((1, 2 * LANES), lambda i: (0, 0))],
        out_specs=[pl.BlockSpec((tg, LANES), lambda i: (i, 0)),
                   pl.BlockSpec((tg, LANES), lambda i: (i, 0)),
                   pl.BlockSpec((heads, tg, LANES), lambda i: (0, i, 0)),
                   pl.BlockSpec((heads, tg, LANES), lambda i: (0, i, 0)),
                   pl.BlockSpec((heads, tg, LANES), lambda i: (0, i, 0)),
                   pl.BlockSpec((LANES, tg), lambda i: (0, i))],
        out_shape=[tok, tok, per_head, per_head, per_head, jax.ShapeDtypeStruct((LANES, m), F32)],
        compiler_params=_cparams("parallel"),
        name="gates",
    )(xb, w, bias)


def _mlstm_prompt_kernel(q_ref, k_ref, v_ref, om_ref, bq_ref, cmq_ref, rq_ref, rrow_ref, g_ref,
                         h_ref, c_ref, n_ref, m_ref, *, chunk, nchunk):
    L = chunk

    @pl.when(pl.program_id(2) == 0)
    def _():
        c_ref[...] = jnp.zeros_like(c_ref)
        n_ref[...] = jnp.zeros_like(n_ref)
        m_ref[...] = jnp.zeros_like(m_ref)

    causal = (lax.broadcasted_iota(jnp.int32, (L, L), 1) <= lax.broadcasted_iota(jnp.int32, (L, L), 0))
    g = g_ref[...]

    def body(c, carry):
        sl = pl.ds(pl.multiple_of(c * L, L), L)
        q = q_ref[sl, :]
        k = k_ref[sl, :]
        v = v_ref[sl, :]
        bq = bq_ref[sl, :]
        cmq = cmq_ref[sl, :]
        rq = rq_ref[sl, :]
        rrow = rrow_ref[pl.ds(c, 1), :]
        cmat = c_ref[...]
        n = n_ref[...]
        m = m_ref[...]
        u = -jnp.maximum(m, cmq)
        p = jnp.exp(jnp.where(causal, rrow + u[:, :L], -jnp.inf))
        s = _dot_nt(q, k) * p
        w_inter = jnp.exp(m + u)
        qf = q.astype(F32)
        num = _dot(s.astype(BF16), v) + w_inter * _dot_nt(q, cmat.astype(BF16))
        den = (jnp.sum(s, axis=-1, keepdims=True)
               + w_inter[:, :1] * jnp.sum(qf * n, axis=-1, keepdims=True))
        hh = num / jnp.maximum(jnp.abs(den), jnp.exp(u - bq))
        out = _head_norm(hh, g) * _sigmoid(om_ref[sl, :])
        h_ref[sl, :] = out.astype(h_ref.dtype)
        mx = jnp.maximum(m, cmq[L - 1:L, :])
        w_old = jnp.exp(m - mx)
        w_end = jnp.exp(rq - mx)
        vw = (w_end * v.astype(F32)).astype(BF16)
        c_ref[...] = w_old * cmat + _dot_tn(vw, k)
        n_ref[...] = w_old * n + jnp.sum(w_end * k.astype(F32), axis=0, keepdims=True)
        m_ref[...] = bq[L - 1:L, :] + mx
        return carry

    lax.fori_loop(0, nchunk, body, 0)


def mlstm_prompt(qm, km, vm, om, bq, cmq, rq, rrow, g_m, *, batch, seq, heads, chunk, tb):
    nt = seq // tb
    nchunk = tb // chunk
    d = HEAD_DIM
    tok = lambda b, h, t: (b * nt + t, h)
    per_head = lambda b, h, t: (h, b * nt + t, 0)
    state = lambda b, h, t: (b, h, 0, 0)
    return pl.pallas_call(
        functools.partial(_mlstm_prompt_kernel, chunk=chunk, nchunk=nchunk),
        grid=(batch, heads, nt),
        in_specs=[pl.BlockSpec((tb, d), tok), pl.BlockSpec((tb, d), tok), pl.BlockSpec((tb, d), tok),
                  pl.BlockSpec((tb, d), tok),
                  pl.BlockSpec((None, tb, LANES), per_head), pl.BlockSpec((None, tb, LANES), per_head),
                  pl.BlockSpec((None, tb, LANES), per_head),
                  pl.BlockSpec((None, nchunk, chunk), per_head),
                  pl.BlockSpec((1, d), lambda b, h, t: (0, h))],
        out_specs=[pl.BlockSpec((tb, d), tok),
                   pl.BlockSpec((None, None, d, d), state),
                   pl.BlockSpec((None, None, 1, d), state),
                   pl.BlockSpec((None, None, 1, LANES), state)],
        out_shape=[jax.ShapeDtypeStruct((batch * seq, heads * d), BF16),
                   jax.ShapeDtypeStruct((batch, heads, d, d), F32),
                   jax.ShapeDtypeStruct((batch, heads, 1, d), F32),
                   jax.ShapeDtypeStruct((batch, heads, 1, LANES), F32)],
        compiler_params=_cparams("parallel", "parallel", "arbitrary"),
        name="mlstm_prompt",
    )(qm, km, vm, om, bq, cmq, rq, rrow, g_m)


def _mlstm_sample_kernel(q_ref, k_ref, v_ref, om_ref, ig_ref, lf_ref, g_ref, c_ref, n_ref, m_ref,
                         h_ref, co_ref, no_ref, mo_ref, *, nb, steps, heads):
    T = steps
    R = nb * T
    qb = q_ref[...]
    kb = k_ref[...]
    qa = qb.astype(F32)
    ka = kb.astype(F32)
    va = v_ref[...].astype(F32)
    sig_o = _sigmoid(om_ref[...])
    iga = ig_ref[...]
    lfa = lf_ref[...]
    gall = g_ref[...]
    t_idx = lax.broadcasted_iota(jnp.int32, (T, LANES), 0)
    row_idx = lax.broadcasted_iota(jnp.int32, (R, LANES), 0)
    for h in range(heads):
        lanes = slice(h * HEAD_DIM, (h + 1) * HEAD_DIM)
        qh_b = qb[:, lanes]
        kh_b = kb[:, lanes]
        g = gall[:, lanes]
        for bb in range(nb):
            rows = slice(bb * T, (bb + 1) * T)
            qf = qa[rows, lanes]
            kf = ka[rows, lanes]
            vf = va[rows, lanes]
            ig_c = jnp.broadcast_to(iga[rows, h:h + 1], (T, LANES))
            lf_c = jnp.broadcast_to(lfa[rows, h:h + 1], (T, LANES))
            cmat = c_ref[bb, h]
            n = n_ref[bb, h:h + 1, :]
            m = m_ref[bb, h:h + 1, :]
            b_c = jnp.zeros((T, LANES), F32)
            for j in range(T):
                b_c = b_c + jnp.where(t_idx >= j, lf_c[j:j + 1, :], 0.0)
            r_c = ig_c - b_c
            cm_c = jnp.full((T, LANES), -jnp.inf, F32)
            for j in range(T):
                cm_c = jnp.maximum(cm_c, jnp.where(t_idx >= j, r_c[j:j + 1, :], -jnp.inf))
            u = -jnp.maximum(m, cm_c)
            w_inter = jnp.exp(m + u)
            num = jnp.zeros((T, HEAD_DIM), F32)
            den = jnp.zeros((T, LANES), F32)
            for s in range(T):
                qk = jnp.sum(qf * kf[s:s + 1, :], axis=-1, keepdims=True)
                coef = jnp.where(t_idx >= s, qk * jnp.exp(r_c[s:s + 1, :] + u), 0.0)
                num = num + coef * vf[s:s + 1, :]
                den = den + coef
            qc = _dot_nt(qh_b, cmat.astype(BF16))[rows, :]
            num = num + w_inter * qc
            den = den + w_inter * jnp.sum(qf * n, axis=-1, keepdims=True)
            hh = num / jnp.maximum(jnp.abs(den), jnp.exp(u - b_c))
            h_ref[rows, lanes] = _head_norm(hh, g) * sig_o[rows, lanes]
            mx = jnp.maximum(m, cm_c[T - 1:T, :])
            w_old = jnp.exp(m - mx)
            w_end = jnp.exp(r_c - mx)
            vw = jnp.zeros((R, HEAD_DIM), F32)
            for s in range(T):
                vw = vw + jnp.where(row_idx == bb * T + s, w_end[s:s + 1, :] * vf[s:s + 1, :], 0.0)
            co_ref[bb, h] = w_old * cmat + _dot_tn(vw.astype(BF16), kh_b)
            no_ref[bb, h:h + 1, :] = w_old * n + jnp.sum(w_end * kf, axis=0, keepdims=True)
            mo_ref[bb, h:h + 1, :] = b_c[T - 1:T, :] + mx


def mlstm_sample(qm, km, vm, om, ig, lf, g_m, c0, n0, m0, *, row0, nb, steps, heads):
    dec_batch = c0.shape[0]
    d = HEAD_DIM
    r = nb * steps
    assert row0 % r == 0 and dec_batch % nb == 0
    blk0 = row0 // r
    tok = lambda i: (blk0 + i, 0)
    st4 = lambda i: (i, 0, 0, 0)
    st3 = lambda i: (i, 0, 0)
    return pl.pallas_call(
        functools.partial(_mlstm_sample_kernel, nb=nb, steps=steps, heads=heads),
        grid=(dec_batch // nb,),
        in_specs=[pl.BlockSpec((r, heads * d), tok), pl.BlockSpec((r, heads * d), tok),
                  pl.BlockSpec((r, heads * d), tok), pl.BlockSpec((r, heads * d), tok),
                  pl.BlockSpec((r, LANES), tok), pl.BlockSpec((r, LANES), tok),
                  pl.BlockSpec((1, heads * d), lambda i: (0, 0)),
                  pl.BlockSpec((nb, heads, d, d), st4),
                  pl.BlockSpec((nb, heads, d), st3),
                  pl.BlockSpec((nb, heads, LANES), st3)],
        out_specs=[pl.BlockSpec((r, heads * d), lambda i: (i, 0)),
                   pl.BlockSpec((nb, heads, d, d), st4),
                   pl.BlockSpec((nb, heads, d), st3),
                   pl.BlockSpec((nb, heads, LANES), st3)],
        out_shape=[jax.ShapeDtypeStruct((dec_batch * steps, heads * d), F32),
                   jax.ShapeDtypeStruct((dec_batch, heads, d, d), F32),
                   jax.ShapeDtypeStruct((dec_batch, heads, d), F32),
                   jax.ShapeDtypeStruct((dec_batch, heads, LANES), F32)],
        compiler_params=_cparams("parallel"),
        name="mlstm_sample",
    )(qm, km, vm, om, ig, lf, g_m, c0, n0, m0)


def _suffix_matrix(sc):
    scp = -(-sc // LANES) * LANES
    j = np.arange(sc)[:, None]
    s = np.arange(scp + LANES)[None, :]
    t = np.where(s < sc, j > s, s >= scp)
    return jnp.asarray(t, dtype=BF16)


def _sb_tile(z, mask, carried, trix, sc):
    scp = trix.shape[1] - LANES
    sp = _softplus_neg_abs(z)
    ls = jnp.minimum(z, 0.0) - sp
    l1 = -jnp.maximum(z, 0.0) - sp
    if mask is not None:
        l1 = jnp.where(mask, l1, 0.0)
    hi = l1.astype(BF16)
    lo = (l1 - hi.astype(F32)).astype(BF16)
    res = _dot(hi, trix) + _dot(lo, trix)
    total = res[:, scp:]
    reps = -(-sc // LANES)
    car = carried if reps == 1 else jnp.concatenate([carried] * reps, axis=1)
    a = jnp.exp(ls + res[:, :sc] + car[:, :sc])
    if mask is not None:
        a = jnp.where(mask, a, 0.0)
    return a, carried + total


def _sb_prompt_kernel(qi_tab, kj_tab, q_ref, k_ref, v_ref, bias_ref, g_ref, trix_ref, o_ref, acc_ref, car_ref,
                      *, tq, tk, sc, scale):
    p = pl.program_id(2)
    qi = qi_tab[p]
    kj = kj_tab[p]

    @pl.when(kj == qi)
    def _():
        acc_ref[...] = jnp.zeros_like(acc_ref)
        car_ref[...] = jnp.zeros_like(car_ref)

    q = q_ref[...]
    bias = bias_ref[...][:, :1]
    trix = trix_ref[...]
    row = qi * tq + lax.broadcasted_iota(jnp.int32, (tq, sc), 0)
    col0 = kj * tk + lax.broadcasted_iota(jnp.int32, (tq, sc), 1)
    acc = acc_ref[...]
    car = car_ref[...]
    for sub in reversed(range(tk // sc)):
        ks = k_ref[sub * sc:(sub + 1) * sc, :]
        vs = v_ref[sub * sc:(sub + 1) * sc, :]
        mask = (col0 + sub * sc) < row
        z = _dot_nt(q, ks) * scale + bias
        a, car = _sb_tile(z, mask, car, trix, sc)
        acc = acc + _dot(a.astype(BF16), vs)
    acc_ref[...] = acc
    car_ref[...] = car

    @pl.when(kj == 0)
    def _():
        o_ref[...] = _head_norm(acc, g_ref[...]).astype(o_ref.dtype)


def sb_prompt(qs, ksb, vsb, bias, g_s, *, batch, seq, heads, tq, sc):
    tk = tq
    nq = seq // tq
    d = HEAD_DIM
    pairs = [(i, j) for i in range(nq) for j in range(i, -1, -1)]
    qi_tab = jnp.asarray([p[0] for p in pairs], jnp.int32)
    kj_tab = jnp.asarray([p[1] for p in pairs], jnp.int32)
    trix = _suffix_matrix(sc)
    grid_spec = pltpu.PrefetchScalarGridSpec(
        num_scalar_prefetch=2,
        grid=(batch, heads, len(pairs)),
        in_specs=[pl.BlockSpec((tq, d), lambda b, h, p, qt, kt: (b * nq + qt[p], h)),
                  pl.BlockSpec((tk, d), lambda b, h, p, qt, kt: (b * nq + kt[p], h)),
                  pl.BlockSpec((tk, d), lambda b, h, p, qt, kt: (b * nq + kt[p], h)),
                  pl.BlockSpec((None, 1, LANES), lambda b, h, p, qt, kt: (h, 0, 0)),
                  pl.BlockSpec((1, d), lambda b, h, p, qt, kt: (0, h)),
                  pl.BlockSpec(trix.shape, lambda b, h, p, qt, kt: (0, 0))],
        out_specs=pl.BlockSpec((tq, d), lambda b, h, p, qt, kt: (b * nq + qt[p], h)),
        scratch_shapes=[pltpu.VMEM((tq, d), F32), pltpu.VMEM((tq, LANES), F32)],
    )
    return pl.pallas_call(
        functools.partial(_sb_prompt_kernel, tq=tq, tk=tk, sc=sc, scale=HEAD_DIM ** -0.5),
        grid_spec=grid_spec,
        out_shape=jax.ShapeDtypeStruct((batch * seq, heads * d), BF16),
        compiler_params=_cparams("parallel", "parallel", "arbitrary"),
        name="sb_prompt",
    )(qi_tab, kj_tab, qs, ksb, vsb, bias, g_s, trix)


def _sb_sample_kernel(pt_ref, q_ref, kn_ref, vn_ref, bias_ref, g_ref, trix_ref, trixn_ref, *rest,
                      pg, steps, heads, nj, scale, page, newpad):
    kp_refs = rest[:pg]
    vp_refs = rest[pg:2 * pg]
    o_ref, qbd_ref, acc_ref, car_ref = rest[2 * pg:]
    j = pl.program_id(1)
    rows = heads * steps
    width = heads * HEAD_DIM
    row_head = lax.broadcasted_iota(jnp.int32, (rows, width), 0) // steps
    bias = bias_ref[...]

    @pl.when(j == 0)
    def _():
        q = q_ref[...]
        tiled = jnp.concatenate([q] * heads, axis=1)
        lane_head = lax.broadcasted_iota(jnp.int32, (rows, width), 1) // HEAD_DIM
        qbd = jnp.where(lane_head == row_head, tiled, jnp.zeros_like(tiled))
        qbd_ref[...] = qbd
        z = _dot_nt(qbd, kn_ref[...]) * scale + bias[:, :newpad]
        t_of_row = lax.broadcasted_iota(jnp.int32, (rows, newpad), 0) % steps
        mask = lax.broadcasted_iota(jnp.int32, (rows, newpad), 1) < t_of_row
        a, car = _sb_tile(z, mask, jnp.zeros((rows, LANES), F32), trixn_ref[...], newpad)
        acc_ref[...] = _dot(a.astype(BF16), vn_ref[...])
        car_ref[...] = car

    qbd = qbd_ref[...]
    trix = trix_ref[...]
    acc = acc_ref[...]
    car = car_ref[...]
    for i in range(pg):
        kp = kp_refs[i][...].astype(BF16)
        vp = vp_refs[i][...].astype(BF16)
        z = _dot_nt(qbd, kp) * scale + bias
        a, car = _sb_tile(z, None, car, trix, page)
        acc = acc + _dot(a.astype(BF16), vp)
    acc_ref[...] = acc
    car_ref[...] = car

    @pl.when(j == nj - 1)
    def _():
        rh = row_head[:, :HEAD_DIM]
        out = jnp.zeros((rows, HEAD_DIM), F32)
        for h in range(heads):
            out = out + jnp.where(rh == h, acc[:, h * HEAD_DIM:(h + 1) * HEAD_DIM], 0.0)
        o_ref[...] = _head_norm(out, g_ref[...])


def sb_sample(q_rows, k_new, v_new, bias_rows, g_rows, cache_k, cache_v, page_table, *, layer, pg, steps, heads):
    bd, rows, d = q_rows.shape
    newpad = k_new.shape[1]
    n_pages = page_table.shape[0] // bd
    page = cache_k.shape[1]
    width = heads * d
    assert n_pages % pg == 0 and page == LANES
    nj = n_pages // pg
    trix = _suffix_matrix(page)
    trixn = _suffix_matrix(newpad)

    def page_map(i):
        def index(b, j, pt):
            return (pt[b * n_pages + (n_pages - 1 - (j * pg + i))], 0, 0)
        return index

    per_b = lambda b, j, pt: (b, 0, 0)
    const2 = lambda b, j, pt: (0, 0)
    in_specs = [pl.BlockSpec((None, rows, d), per_b),
                pl.BlockSpec((None, newpad, width), per_b),
                pl.BlockSpec((None, newpad, width), per_b),
                pl.BlockSpec((rows, LANES), const2),
                pl.BlockSpec((rows, d), const2),
                pl.BlockSpec(trix.shape, const2),
                pl.BlockSpec(trixn.shape, const2)]
    in_specs += [pl.BlockSpec((None, page, width), page_map(i)) for i in range(pg)]
    in_specs += [pl.BlockSpec((None, page, width), page_map(i)) for i in range(pg)]
    grid_spec = pltpu.PrefetchScalarGridSpec(
        num_scalar_prefetch=1,
        grid=(bd, nj),
        in_specs=in_specs,
        out_specs=pl.BlockSpec((None, rows, d), per_b),
        scratch_shapes=[pltpu.VMEM((rows, width), BF16), pltpu.VMEM((rows, width), F32),
                        pltpu.VMEM((rows, LANES), F32)],
    )
    return pl.pallas_call(
        functools.partial(_sb_sample_kernel, pg=pg, steps=steps, heads=heads, nj=nj,
                          scale=HEAD_DIM ** -0.5, page=page, newpad=newpad),
        grid_spec=grid_spec,
        out_shape=jax.ShapeDtypeStruct((bd, rows, d), F32),
        compiler_params=_cparams("parallel", "arbitrary"),
        name="sb_sample",
    )(page_table + layer, q_rows, k_new, v_new, bias_rows, g_rows, trix, trixn,
      *([cache_k] * pg), *([cache_v] * pg))


def kernel(x_prompt, x_sample, state_mlstm_C, state_mlstm_n, state_mlstm_m, cache_sb_k, cache_sb_v, page_table,
           p_prompt, p_sample, w_ffn1_gu, w_ffn1_dn, w_in, b_if, b_sb, w_out, g_head_m, g_head_s, w_ffn2_gu,
           w_ffn2_dn, w_ple_gate, w_ple_proj, ln_g, ln_b):
    batch, seq, d_model = x_prompt.shape
    dec_batch, dec_seq, _ = x_sample.shape
    depth = w_in.shape[0]
    h_m = state_mlstm_C.shape[2]
    h_s = cache_sb_k.shape[3]
    n_phys, page = cache_sb_k.shape[1], cache_sb_k.shape[2]
    d = HEAD_DIM
    w_m, w_s = h_m * d, h_s * d
    assert w_m == w_s
    d_ff = w_ffn1_dn.shape[1]
    alpha = (2 * depth) ** 0.25
    mp = batch * seq
    ms = dec_batch * dec_seq
    m = mp + ms
    chunk = MLSTM_CHUNK

    tm_big = _row_tile(m, 1088)
    tm_ln = _row_tile(m, 544)
    tf = 512
    fp = -(-d_ff // tf) * tf
    tk_dn = 512
    tg = _row_tile(m, 512, mult=chunk)
    tb = min(seq, 1024)
    tq = min(seq, 512)
    sc = 256
    pg = 4
    nb = 4
    newpad = 16

    def prep_gu(w):
        w = w.reshape(depth, d_model, 2, d_ff)
        w = jnp.pad(w, ((0, 0), (0, 0), (0, 0), (0, fp - d_ff)))
        return jnp.transpose(w, (0, 2, 1, 3)).astype(BF16)

    def prep_dn(w):
        return jnp.pad(w, ((0, 0), (0, fp - d_ff), (0, 0))).astype(BF16)

    wgu1, wgu2 = prep_gu(w_ffn1_gu), prep_gu(w_ffn2_gu)
    wdn1, wdn2 = prep_dn(w_ffn1_dn), prep_dn(w_ffn2_dn)
    g0 = 4 * w_m
    win_main = jnp.concatenate([w_in[:, :, :g0], w_in[:, :, g0 + 2 * h_m:]], axis=-1).astype(BF16)
    zpad = jnp.zeros((depth, d_model, LANES - h_m), w_in.dtype)
    win_gate = jnp.concatenate([w_in[:, :, g0:g0 + h_m], zpad, w_in[:, :, g0 + h_m:g0 + 2 * h_m], zpad],
                               axis=-1).astype(BF16)
    bpad = jnp.zeros((depth, LANES - h_m), F32)
    bias_gate = jnp.concatenate([b_if[:, :h_m], bpad, b_if[:, h_m:], bpad], axis=-1).reshape(depth, 1, 2 * LANES)
    wout = w_out.astype(BF16)
    wpg = w_ple_gate.astype(BF16)
    wpp = w_ple_proj.astype(BF16)
    ck = cache_sb_k.reshape(depth * n_phys, page, w_s)
    cv = cache_sb_v.reshape(depth * n_phys, page, w_s)
    pt_flat = page_table.reshape(-1).astype(jnp.int32)

    x = jnp.concatenate([x_prompt.reshape(mp, d_model), x_sample.reshape(ms, d_model)], axis=0)
    xb = x.astype(BF16)
    p_all = jnp.concatenate([p_prompt.reshape(depth, mp, -1), p_sample.reshape(depth, ms, -1)], axis=1)

    outs = {k: [] for k in ("pC", "pn", "pm", "pk", "pv", "sC", "sn", "sm", "sk", "sv")}
    for l in range(depth):
        lg = ln_g[l].reshape(4, 1, d_model)
        lb = ln_b[l].reshape(4, 1, d_model)

        hmid = ffn_up(xb, wgu1[l], tm=tm_big, tf=tf)
        x, xb = mm_res_ln(hmid, wdn1[l], x, lg[0], lb[0], alpha=alpha, scale=0.5, tm=tm_ln, tk=tk_dn)

        qm, km, vm, om, qs, ks, ksb, vs, vsb = proj(xb, win_main[l], width=w_m, tm=tm_ln)
        ig, lf, bq, cmq, rq, rt = gates(xb, win_gate[l], bias_gate[l], heads=h_m, chunk=chunk, tg=tg)
        rrow = rt[:h_m, :mp].reshape(h_m, mp // chunk, chunk)
        g_m = g_head_m[l].reshape(1, w_m)
        g_s = g_head_s[l].reshape(1, w_s)

        hm_p, pC, pn, pm = mlstm_prompt(qm, km, vm, om, bq, cmq, rq, rrow, g_m,
                                        batch=batch, seq=seq, heads=h_m, chunk=chunk, tb=tb)
        m0 = jnp.broadcast_to(state_mlstm_m[l][:, :, None], (dec_batch, h_m, LANES))
        hm_s, sC, sn, sm = mlstm_sample(qm, km, vm, om, ig, lf, g_m, state_mlstm_C[l], state_mlstm_n[l], m0,
                                        row0=mp, nb=nb, steps=dec_seq, heads=h_m)

        bias_h = jnp.broadcast_to(b_sb[l].astype(F32)[:, None, None], (h_s, 1, LANES))
        hs_p = sb_prompt(qs, ksb, vsb, bias_h, g_s, batch=batch, seq=seq, heads=h_s, tq=tq, sc=sc)

        rows = h_s * dec_seq
        q_rows = qs[mp:].reshape(dec_batch, dec_seq, h_s, d).transpose(0, 2, 1, 3).reshape(dec_batch, rows, d)
        kn = jnp.pad(ksb[mp:].reshape(dec_batch, dec_seq, w_s), ((0, 0), (0, newpad - dec_seq), (0, 0)))
        vn = jnp.pad(vsb[mp:].reshape(dec_batch, dec_seq, w_s), ((0, 0), (0, newpad - dec_seq), (0, 0)))
        bias_rows = jnp.broadcast_to(jnp.repeat(b_sb[l].astype(F32), dec_seq)[:, None], (rows, LANES))
        g_rows = jnp.repeat(g_head_s[l].reshape(h_s, d), dec_seq, axis=0)
        hs_s = sb_sample(q_rows, kn, vn, bias_rows, g_rows, ck, cv, pt_flat, layer=l * n_phys, pg=pg,
                         steps=dec_seq, heads=h_s)
        hs_s = hs_s.reshape(dec_batch, h_s, dec_seq, d).transpose(0, 2, 1, 3).reshape(ms, w_s)

        mix = jnp.concatenate(
            [jnp.concatenate([hm_p, hs_p], axis=1),
             jnp.concatenate([hm_s, hs_s], axis=1).astype(BF16)], axis=0)
        x, xb = mm_res_ln(mix, wout[l], x, lg[1], lb[1], alpha=alpha, scale=1.0, tm=tm_ln, tk=512)

        hmid = ffn_up(xb, wgu2[l], tm=tm_big, tf=tf)
        x, xb = mm_res_ln(hmid, wdn2[l], x, lg[2], lb[2], alpha=alpha, scale=0.5, tm=tm_ln, tk=tk_dn)

        x, xb = ple(xb, wpg[l], p_all[l], wpp[l], x, lg[3], lb[3], alpha=alpha, tm=tm_ln, tk=512)

        outs["pC"].append(pC)
        outs["pn"].append(pn.reshape(batch, h_m, d))
        outs["pm"].append(pm[:, :, 0, 0])
        outs["pk"].append(ks[:mp].reshape(batch, seq, h_s, d))
        outs["pv"].append(vs[:mp].reshape(batch, seq, h_s, d))
        outs["sC"].append(sC)
        outs["sn"].append(sn)
        outs["sm"].append(sm[:, :, 0])
        outs["sk"].append(ks[mp:].reshape(dec_batch, dec_seq, h_s, d))
        outs["sv"].append(vs[mp:].reshape(dec_batch, dec_seq, h_s, d))

    st = lambda k: jnp.stack(outs[k], axis=0)
    return (x[:mp].reshape(batch, seq, d_model), x[mp:].reshape(dec_batch, dec_seq, d_model),
            st("pC"), st("pn"), st("pm"), st("pk"), st("pv"),
            st("sC"), st("sn"), st("sm"), st("sk"), st("sv"))
```

```python
import functools

import numpy as np
import jax
import jax.numpy as jnp
from jax import lax
from jax.experimental import pallas as pl
from jax.experimental.pallas import tpu as pltpu

F32 = jnp.float32
BF16 = jnp.bfloat16

HEAD_DIM = 128
LANES = 128
SUBLANES = 8
BF16_ROWS = 16
LN_EPS = 1e-5
MLSTM_CHUNK = 128
VMEM_LIMIT = 56 * 1024 * 1024


def _cparams(*sem):
    return pltpu.CompilerParams(dimension_semantics=sem, vmem_limit_bytes=VMEM_LIMIT)


def _dot(a, b):
    return jnp.dot(a, b, preferred_element_type=F32)


def _dot_nt(a, b):
    return lax.dot_general(a, b, (((1,), (1,)), ((), ())), preferred_element_type=F32)


def _dot_tn(a, b):
    return lax.dot_general(a, b, (((0,), (0,)), ((), ())), preferred_element_type=F32)


def _row_tile(m, target, mult=16):
    best = None
    for t in range(mult, min(m, target) + 1, mult):
        if m % t == 0:
            best = t
    assert best is not None, (m, target)
    return best


def _layer_norm(y, g, b):
    mu = jnp.mean(y, axis=-1, keepdims=True)
    yc = y - mu
    var = jnp.mean(yc * yc, axis=-1, keepdims=True)
    return yc * lax.rsqrt(var + LN_EPS) * g + b


def _head_norm(h, g):
    mu = jnp.mean(h, axis=-1, keepdims=True)
    hc = h - mu
    var = jnp.mean(hc * hc, axis=-1, keepdims=True)
    return hc * lax.rsqrt(var + LN_EPS) * g


def _sigmoid(x):
    return 1.0 / (1.0 + jnp.exp(-x))


def _cast_gu_kernel(w_ref, g_ref, u_ref, *, d_ff, fp):
    w = w_ref[...]
    g_ref[:, :d_ff] = w[:, :d_ff].astype(BF16)
    u_ref[:, :d_ff] = w[:, d_ff:].astype(BF16)
    if fp > d_ff:
        zeros = jnp.zeros((w.shape[0], fp - d_ff), BF16)
        g_ref[:, d_ff:] = zeros
        u_ref[:, d_ff:] = zeros


def cast_gu(w, *, layer, fp, tr):
    _, d, two_ff = w.shape
    d_ff = two_ff // 2
    assert d_ff % LANES == 0 and fp % LANES == 0 and d % tr == 0
    out = jax.ShapeDtypeStruct((d, fp), BF16)
    return pl.pallas_call(
        functools.partial(_cast_gu_kernel, d_ff=d_ff, fp=fp),
        grid=(d // tr,),
        in_specs=[pl.BlockSpec((None, tr, two_ff), lambda i: (layer, i, 0))],
        out_specs=[pl.BlockSpec((tr, fp), lambda i: (i, 0)), pl.BlockSpec((tr, fp), lambda i: (i, 0))],
        out_shape=[out, out],
        compiler_params=_cparams("parallel"),
        name="cast_gu",
    )(w)


def _ffn_up_kernel(x_ref, wg_ref, wu_ref, h_ref):
    x = x_ref[...]
    a = _dot(x, wg_ref[...])
    b = _dot(x, wu_ref[...])
    h_ref[...] = (a * _sigmoid(a) * b).astype(h_ref.dtype)


def ffn_up(xb, wg, wu, *, tm, tf):
    m, d = xb.shape
    fp = wg.shape[1]
    return pl.pallas_call(
        _ffn_up_kernel,
        grid=(m // tm, fp // tf),
        in_specs=[pl.BlockSpec((tm, d), lambda i, j: (i, 0)),
                  pl.BlockSpec((d, tf), lambda i, j: (0, j)),
                  pl.BlockSpec((d, tf), lambda i, j: (0, j))],
        out_specs=pl.BlockSpec((tm, tf), lambda i, j: (i, j)),
        out_shape=jax.ShapeDtypeStruct((m, fp), BF16),
        compiler_params=_cparams("parallel", "arbitrary"),
        name="ffn_up",
    )(xb, wg, wu)


def _ln_from_slabs(pre_ref, g_ref, b_ref, y_ref, yb_ref, nj, tn):
    n = nj * tn
    tot = jnp.sum(pre_ref[0], axis=-1, keepdims=True)
    for j in range(1, nj):
        tot = tot + jnp.sum(pre_ref[j], axis=-1, keepdims=True)
    mu = tot / n
    sq = jnp.zeros_like(mu)
    for j in range(nj):
        c = pre_ref[j] - mu
        sq = sq + jnp.sum(c * c, axis=-1, keepdims=True)
    inv = lax.rsqrt(sq / n + LN_EPS)
    for j in range(nj):
        cols = slice(j * tn, (j + 1) * tn)
        y = (pre_ref[j] - mu) * inv * g_ref[:, cols] + b_ref[:, cols]
        y_ref[:, cols] = y
        yb_ref[:, cols] = y.astype(BF16)


def _ffn_down_kernel(h_ref, w_ref, x_ref, g_ref, b_ref, y_ref, yb_ref, pre_ref, *, alpha, scale, nj, tn, d_ff):
    j = pl.program_id(1)
    pre_ref[j] = alpha * x_ref[...] + scale * _dot(h_ref[:, :d_ff], w_ref[...])

    @pl.when(j == nj - 1)
    def _():
        _ln_from_slabs(pre_ref, g_ref, b_ref, y_ref, yb_ref, nj, tn)


def ffn_down(h, w, x, g, b, *, layer, alpha, scale, tm, tn):
    m, kdim = h.shape
    _, d_ff, d = w.shape
    assert d_ff <= kdim and d_ff % LANES == 0
    nj = d // tn
    return pl.pallas_call(
        functools.partial(_ffn_down_kernel, alpha=alpha, scale=scale, nj=nj, tn=tn, d_ff=d_ff),
        grid=(m // tm, nj),
        in_specs=[pl.BlockSpec((tm, kdim), lambda i, j: (i, 0)),
                  pl.BlockSpec((None, d_ff, tn), lambda i, j: (layer, 0, j)),
                  pl.BlockSpec((tm, tn), lambda i, j: (i, j)),
                  pl.BlockSpec((1, d), lambda i, j: (0, 0)),
                  pl.BlockSpec((1, d), lambda i, j: (0, 0))],
        out_specs=[pl.BlockSpec((tm, d), lambda i, j: (i, 0)),
                   pl.BlockSpec((tm, d), lambda i, j: (i, 0))],
        out_shape=[jax.ShapeDtypeStruct((m, d), F32), jax.ShapeDtypeStruct((m, d), BF16)],
        scratch_shapes=[pltpu.VMEM((nj, tm, tn), F32)],
        compiler_params=_cparams("parallel", "arbitrary"),
        name="ffn_down",
    )(h, w, x, g, b)


def _out_proj_kernel(hm_ref, hs_ref, wm_ref, ws_ref, x_ref, g_ref, b_ref, y_ref, yb_ref, *, alpha):
    mix = _dot(hm_ref[...], wm_ref[...]) + _dot(hs_ref[...], ws_ref[...])
    y = _layer_norm(alpha * x_ref[...] + mix, g_ref[...], b_ref[...])
    y_ref[...] = y
    yb_ref[...] = y.astype(BF16)


def out_proj(hm, hs, w, x, g, b, *, layer, alpha, tm):
    m, wm = hm.shape
    ws = hs.shape[1]
    d = w.shape[2]
    assert wm == ws and w.shape[1] == wm + ws
    row = lambda i: (i, 0)
    return pl.pallas_call(
        functools.partial(_out_proj_kernel, alpha=alpha),
        grid=(m // tm,),
        in_specs=[pl.BlockSpec((tm, wm), row), pl.BlockSpec((tm, ws), row),
                  pl.BlockSpec((None, wm, d), lambda i: (layer, 0, 0)),
                  pl.BlockSpec((None, ws, d), lambda i: (layer, 1, 0)),
                  pl.BlockSpec((tm, d), row),
                  pl.BlockSpec((1, d), lambda i: (0, 0)), pl.BlockSpec((1, d), lambda i: (0, 0))],
        out_specs=[pl.BlockSpec((tm, d), row), pl.BlockSpec((tm, d), row)],
        out_shape=[jax.ShapeDtypeStruct((m, d), F32), jax.ShapeDtypeStruct((m, d), BF16)],
        compiler_params=_cparams("parallel"),
        name="out_proj",
    )(hm, hs, w, w, x, g, b)


def _ple_kernel(xb_ref, wg_ref, p_ref, wp_ref, x_ref, g_ref, b_ref, y_ref, yb_ref, *, alpha):
    gate = _sigmoid(_dot(xb_ref[...], wg_ref[...]))
    proj = _dot(p_ref[...].astype(BF16), wp_ref[...])
    y = _layer_norm(alpha * x_ref[...] + gate * proj, g_ref[...], b_ref[...])
    y_ref[...] = y
    yb_ref[...] = y.astype(BF16)


def ple(xb, wg, p, wp, x, g, b, *, layer, alpha, tm):
    m, d = xb.shape
    pd = p.shape[2]
    row = lambda i: (i, 0)
    const = lambda i: (0, 0)
    lconst = lambda i: (layer, 0, 0)
    return pl.pallas_call(
        functools.partial(_ple_kernel, alpha=alpha),
        grid=(m // tm,),
        in_specs=[pl.BlockSpec((tm, d), row), pl.BlockSpec((None, d, d), lconst),
                  pl.BlockSpec((None, tm, pd), lambda i: (layer, i, 0)), pl.BlockSpec((None, pd, d), lconst),
                  pl.BlockSpec((tm, d), row),
                  pl.BlockSpec((1, d), const), pl.BlockSpec((1, d), const)],
        out_specs=[pl.BlockSpec((tm, d), row), pl.BlockSpec((tm, d), row)],
        out_shape=[jax.ShapeDtypeStruct((m, d), F32), jax.ShapeDtypeStruct((m, d), BF16)],
        compiler_params=_cparams("parallel"),
        name="ple",
    )(xb, wg, p, wp, x, g, b)


def _proj_m_kernel(x_ref, w_ref, qm_ref, km_ref, vm_ref, om_ref, *, k_scale):
    j = pl.program_id(1)
    u = _dot(x_ref[...], w_ref[...])

    @pl.when(j == 0)
    def _():
        qm_ref[...] = u.astype(BF16)

    @pl.when(j == 1)
    def _():
        km_ref[...] = (u * k_scale).astype(BF16)

    @pl.when(j == 2)
    def _():
        vm_ref[...] = u.astype(BF16)

    @pl.when(j == 3)
    def _():
        om_ref[...] = u


def proj_m(xb, w, *, layer, width, tm):
    m, d = xb.shape
    dts = [BF16, BF16, BF16, F32]
    return pl.pallas_call(
        functools.partial(_proj_m_kernel, k_scale=HEAD_DIM ** -0.5),
        grid=(m // tm, 4),
        in_specs=[pl.BlockSpec((tm, d), lambda i, j: (i, 0)),
                  pl.BlockSpec((None, d, width), lambda i, j: (layer, 0, j))],
        out_specs=[pl.BlockSpec((tm, width), lambda i, j: (i, 0)) for _ in dts],
        out_shape=[jax.ShapeDtypeStruct((m, width), dt) for dt in dts],
        compiler_params=_cparams("parallel", "arbitrary"),
        name="proj_m",
    )(xb, w)


def _proj_s_kernel(x_ref, w_ref, pk_in, pv_in, sk_in, sv_in, qs_ref, ksb_ref, vsb_ref,
                   pk_ref, pv_ref, sk_ref, sv_ref, *, heads, n_prompt_tiles, tm):
    del pk_in, pv_in, sk_in, sv_in
    i = pl.program_id(0)
    j = pl.program_id(1)
    u = _dot(x_ref[...], w_ref[...])

    def scatter_heads(dst_ref):
        for h in range(heads):
            dst_ref[pl.ds(h, tm, stride=heads), :] = u[:, h * HEAD_DIM:(h + 1) * HEAD_DIM]

    @pl.when(j == 0)
    def _():
        qs_ref[...] = u.astype(BF16)

    @pl.when(j == 1)
    def _():
        ksb_ref[...] = u.astype(BF16)

    @pl.when(j == 2)
    def _():
        vsb_ref[...] = u.astype(BF16)

    @pl.when((j == 1) & (i < n_prompt_tiles))
    def _():
        scatter_heads(pk_ref)

    @pl.when((j == 1) & (i == n_prompt_tiles))
    def _():
        scatter_heads(sk_ref)

    @pl.when((j == 2) & (i < n_prompt_tiles))
    def _():
        scatter_heads(pv_ref)

    @pl.when((j == 2) & (i == n_prompt_tiles))
    def _():
        scatter_heads(sv_ref)


def proj_s(xb, w, pk, pv, sk, sv, *, layer, width, heads, mp, ms):
    m, d = xb.shape
    tm = ms
    assert mp % tm == 0 and m == mp + ms and tm % BF16_ROWS == 0
    npt = mp // tm
    tok = lambda i, j: (i, 0)
    pmap = lambda i, j: (layer, jnp.minimum(i, npt - 1), 0)
    smap = lambda i, j: (layer, 0, 0)
    any_spec = pl.BlockSpec(memory_space=pl.ANY)
    return pl.pallas_call(
        functools.partial(_proj_s_kernel, heads=heads, n_prompt_tiles=npt, tm=tm),
        grid=(m // tm, 3),
        in_specs=[pl.BlockSpec((tm, d), tok),
                  pl.BlockSpec((None, d, width), lambda i, j: (layer, 0, j)),
                  any_spec, any_spec, any_spec, any_spec],
        out_specs=[pl.BlockSpec((tm, width), tok), pl.BlockSpec((tm, width), tok), pl.BlockSpec((tm, width), tok),
                   pl.BlockSpec((None, tm * heads, HEAD_DIM), pmap),
                   pl.BlockSpec((None, tm * heads, HEAD_DIM), pmap),
                   pl.BlockSpec((None, tm * heads, HEAD_DIM), smap),
                   pl.BlockSpec((None, tm * heads, HEAD_DIM), smap)],
        out_shape=[jax.ShapeDtypeStruct((m, width), BF16)] * 3
                  + [jax.ShapeDtypeStruct(a.shape, a.dtype) for a in (pk, pv, sk, sv)],
        input_output_aliases={2: 3, 3: 4, 4: 5, 5: 6},
        compiler_params=_cparams("arbitrary", "arbitrary"),
        name="proj_s",
    )(xb, w, pk, pv, sk, sv)


def _gates_kernel(x_ref, w_ref, bias_ref, ig_ref, lf_ref, bq_ref, cmq_ref, rq_ref, rt_ref, *, chunk, heads):
    u = _dot(x_ref[...], w_ref[...]) + bias_ref[...]
    ig = u[:, :LANES]
    fp = u[:, LANES:]
    lf = jnp.minimum(fp, 0.0) - jnp.log1p(jnp.exp(-jnp.abs(fp)))
    ig_ref[...] = ig
    lf_ref[...] = lf
    tg = ig.shape[0]
    pos = lax.broadcasted_iota(jnp.int32, (tg, LANES), 0) & (chunk - 1)
    b = lf
    d = 1
    while d < chunk:
        b = b + jnp.where(pos >= d, pltpu.roll(b, d, axis=0), 0.0)
        d *= 2
    r = ig - b
    cm = r
    d = 1
    while d < chunk:
        cm = jnp.maximum(cm, jnp.where(pos >= d, pltpu.roll(cm, d, axis=0), -jnp.inf))
        d *= 2
    for h in range(heads):
        bq_ref[h] = jnp.broadcast_to(b[:, h:h + 1], (tg, LANES))
        cmq_ref[h] = jnp.broadcast_to(cm[:, h:h + 1], (tg, LANES))
        rq_ref[h] = jnp.broadcast_to(r[:, h:h + 1], (tg, LANES))
    rt_ref[...] = r.T


def gates(xb, w, bias, *, layer, heads, chunk, tg):
    m, d = xb.shape
    assert tg % chunk == 0 and chunk & (chunk - 1) == 0
    tok = jax.ShapeDtypeStruct((m, LANES), F32)
    per_head = jax.ShapeDtypeStruct((heads, m, LANES), F32)
    return pl.pallas_call(
        functools.partial(_gates_kernel, chunk=chunk, heads=heads),
        grid=(m // tg,),
        in_specs=[pl.BlockSpec((tg, d), lambda i: (i, 0)),
                  pl.BlockSpec((None, d, 2 * LANES), lambda i: (layer, 0, 0)),
                  pl.BlockSpec((None, 1, 2 * LANES), lambda i: (layer, 0, 0))],
        out_specs=[pl.BlockSpec((tg, LANES), lambda i: (i, 0)),
                   pl.BlockSpec((tg, LANES), lambda i: (i, 0)),
                   pl.BlockSpec((heads, tg, LANES), lambda i: (0, i, 0)),
                   pl.BlockSpec((heads, tg, LANES), lambda i: (0, i, 0)),
                   pl.BlockSpec((heads, tg, LANES), lambda i: (0, i, 0)),
                   pl.BlockSpec((LANES, tg), lambda i: (0, i))],
        out_shape=[tok, tok, per_head, per_head, per_head, jax.ShapeDtypeStruct((LANES, m), F32)],
        compiler_params=_cparams("parallel"),
        name="gates",
    )(xb, w, bias)


def _mlstm_prompt_kernel(q_ref, k_ref, v_ref, om_ref, bq_ref, cmq_ref, rq_ref, rrow_ref, g_ref, buf_in,
                         h_ref, c_ref, n_ref, m_ref, *, chunk, nchunk):
    del buf_in
    L = chunk

    @pl.when(pl.program_id(2) == 0)
    def _():
        c_ref[...] = jnp.zeros_like(c_ref)
        n_ref[...] = jnp.zeros_like(n_ref)
        m_ref[...] = jnp.zeros_like(m_ref)

    causal = (lax.broadcasted_iota(jnp.int32, (L, L), 1) <= lax.broadcasted_iota(jnp.int32, (L, L), 0))
    g = g_ref[...]

    def body(c, carry):
        sl = pl.ds(pl.multiple_of(c * L, L), L)
        q = q_ref[sl, :]
        k = k_ref[sl, :]
        v = v_ref[sl, :]
        bq = bq_ref[sl, :]
        cmq = cmq_ref[sl, :]
        rq = rq_ref[sl, :]
        rrow = rrow_ref[pl.ds(c, 1), :]
        cmat = c_ref[...]
        n = n_ref[...]
        m = m_ref[...]
        u = -jnp.maximum(m, cmq)
        p = jnp.exp(jnp.where(causal, rrow + u[:, :L], -jnp.inf))
        s = _dot_nt(q, k) * p
        w_inter = jnp.exp(m + u)
        qf = q.astype(F32)
        num = _dot(s.astype(BF16), v) + w_inter * _dot_nt(q, cmat.astype(BF16))
        den = (jnp.sum(s, axis=-1, keepdims=True)
               + w_inter[:, :1] * jnp.sum(qf * n, axis=-1, keepdims=True))
        hh = num / jnp.maximum(jnp.abs(den), jnp.exp(u - bq))
        out = _head_norm(hh, g) * _sigmoid(om_ref[sl, :])
        h_ref[sl, :] = out.astype(h_ref.dtype)
        mx = jnp.maximum(m, cmq[L - 1:L, :])
        w_old = jnp.exp(m - mx)
        w_end = jnp.exp(rq - mx)
        vw = (w_end * v.astype(F32)).astype(BF16)
        c_ref[...] = w_old * cmat + _dot_tn(vw, k)
        n_ref[...] = w_old * n + jnp.sum(w_end * k.astype(F32), axis=0, keepdims=True)
        m_ref[...] = bq[L - 1:L, :] + mx
        return carry

    lax.fori_loop(0, nchunk, body, 0)


def mlstm_prompt(qm, km, vm, om, bq, cmq, rq, rrow, g_m, buf, *, batch, seq, heads, chunk, tb):
    nt = seq // tb
    nchunk = tb // chunk
    d = HEAD_DIM
    tok = lambda b, h, t: (b * nt + t, h)
    per_head = lambda b, h, t: (h, b * nt + t, 0)
    state = lambda b, h, t: (b, h, 0, 0)
    return pl.pallas_call(
        functools.partial(_mlstm_prompt_kernel, chunk=chunk, nchunk=nchunk),
        grid=(batch, heads, nt),
        in_specs=[pl.BlockSpec((tb, d), tok), pl.BlockSpec((tb, d), tok), pl.BlockSpec((tb, d), tok),
                  pl.BlockSpec((tb, d), tok),
                  pl.BlockSpec((None, tb, LANES), per_head), pl.BlockSpec((None, tb, LANES), per_head),
                  pl.BlockSpec((None, tb, LANES), per_head),
                  pl.BlockSpec((None, nchunk, chunk), per_head),
                  pl.BlockSpec((1, d), lambda b, h, t: (0, h)),
                  pl.BlockSpec(memory_space=pl.ANY)],
        out_specs=[pl.BlockSpec((tb, d), tok),
                   pl.BlockSpec((None, None, d, d), state),
                   pl.BlockSpec((None, None, 1, d), state),
                   pl.BlockSpec((None, None, 1, LANES), state)],
        out_shape=[jax.ShapeDtypeStruct(buf.shape, buf.dtype),
                   jax.ShapeDtypeStruct((batch, heads, d, d), F32),
                   jax.ShapeDtypeStruct((batch, heads, 1, d), F32),
                   jax.ShapeDtypeStruct((batch, heads, 1, LANES), F32)],
        input_output_aliases={9: 0},
        compiler_params=_cparams("parallel", "parallel", "arbitrary"),
        name="mlstm_prompt",
    )(qm, km, vm, om, bq, cmq, rq, rrow, g_m, buf)


def _mlstm_sample_kernel(q_ref, k_ref, v_ref, om_ref, ig_ref, lf_ref, g_ref, c_ref, n_ref, m_ref, co_in,
                         h_ref, co_ref, no_ref, mo_ref, *, nb, steps, heads):
    del co_in
    T = steps
    R = nb * T
    qb = q_ref[...]
    kb = k_ref[...]
    qa = qb.astype(F32)
    ka = kb.astype(F32)
    va = v_ref[...].astype(F32)
    sig_o = _sigmoid(om_ref[...])
    iga = ig_ref[...]
    lfa = lf_ref[...]
    gall = g_ref[...]
    t_idx = lax.broadcasted_iota(jnp.int32, (T, LANES), 0)
    row_idx = lax.broadcasted_iota(jnp.int32, (R, LANES), 0)
    for h in range(heads):
        lanes = slice(h * HEAD_DIM, (h + 1) * HEAD_DIM)
        qh_b = qb[:, lanes]
        kh_b = kb[:, lanes]
        g = gall[:, lanes]
        for bb in range(nb):
            rows = slice(bb * T, (bb + 1) * T)
            qf = qa[rows, lanes]
            kf = ka[rows, lanes]
            vf = va[rows, lanes]
            ig_c = jnp.broadcast_to(iga[rows, h:h + 1], (T, LANES))
            lf_c = jnp.broadcast_to(lfa[rows, h:h + 1], (T, LANES))
            cmat = c_ref[bb, h]
            n = n_ref[bb, h:h + 1, :]
            m = m_ref[bb, h:h + 1, :]
            b_c = jnp.zeros((T, LANES), F32)
            for j in range(T):
                b_c = b_c + jnp.where(t_idx >= j, lf_c[j:j + 1, :], 0.0)
            r_c = ig_c - b_c
            cm_c = jnp.full((T, LANES), -jnp.inf, F32)
            for j in range(T):
                cm_c = jnp.maximum(cm_c, jnp.where(t_idx >= j, r_c[j:j + 1, :], -jnp.inf))
            u = -jnp.maximum(m, cm_c)
            w_inter = jnp.exp(m + u)
            num = jnp.zeros((T, HEAD_DIM), F32)
            den = jnp.zeros((T, LANES), F32)
            for s in range(T):
                qk = jnp.sum(qf * kf[s:s + 1, :], axis=-1, keepdims=True)
                coef = jnp.where(t_idx >= s, qk * jnp.exp(r_c[s:s + 1, :] + u), 0.0)
                num = num + coef * vf[s:s + 1, :]
                den = den + coef
            qc = _dot_nt(qh_b, cmat.astype(BF16))[rows, :]
            num = num + w_inter * qc
            den = den + w_inter * jnp.sum(qf * n, axis=-1, keepdims=True)
            hh = num / jnp.maximum(jnp.abs(den), jnp.exp(u - b_c))
            h_ref[rows, lanes] = _head_norm(hh, g) * sig_o[rows, lanes]
            mx = jnp.maximum(m, cm_c[T - 1:T, :])
            w_old = jnp.exp(m - mx)
            w_end = jnp.exp(r_c - mx)
            vw = jnp.zeros((R, HEAD_DIM), F32)
            for s in range(T):
                vw = vw + jnp.where(row_idx == bb * T + s, w_end[s:s + 1, :] * vf[s:s + 1, :], 0.0)
            co_ref[bb, h] = w_old * cmat + _dot_tn(vw.astype(BF16), kh_b)
            no_ref[bb, h:h + 1, :] = w_old * n + jnp.sum(w_end * kf, axis=0, keepdims=True)
            mo_ref[bb, h:h + 1, :] = b_c[T - 1:T, :] + mx


def mlstm_sample(qm, km, vm, om, ig, lf, g_m, c_all, n0, m0, c_out, *, layer, row0, nb, steps, heads):
    dec_batch = c_all.shape[1]
    d = HEAD_DIM
    r = nb * steps
    assert row0 % r == 0 and dec_batch % nb == 0
    blk0 = row0 // r
    tok = lambda i: (blk0 + i, 0)
    st5 = lambda i: (layer, i, 0, 0, 0)
    st3 = lambda i: (i, 0, 0)
    return pl.pallas_call(
        functools.partial(_mlstm_sample_kernel, nb=nb, steps=steps, heads=heads),
        grid=(dec_batch // nb,),
        in_specs=[pl.BlockSpec((r, heads * d), tok), pl.BlockSpec((r, heads * d), tok),
                  pl.BlockSpec((r, heads * d), tok), pl.BlockSpec((r, heads * d), tok),
                  pl.BlockSpec((r, LANES), tok), pl.BlockSpec((r, LANES), tok),
                  pl.BlockSpec((1, heads * d), lambda i: (0, 0)),
                  pl.BlockSpec((None, nb, heads, d, d), st5),
                  pl.BlockSpec((nb, heads, d), st3),
                  pl.BlockSpec((nb, heads, LANES), st3),
                  pl.BlockSpec(memory_space=pl.ANY)],
        out_specs=[pl.BlockSpec((r, heads * d), lambda i: (i, 0)),
                   pl.BlockSpec((None, nb, heads, d, d), st5),
                   pl.BlockSpec((nb, heads, d), st3),
                   pl.BlockSpec((nb, heads, LANES), st3)],
        out_shape=[jax.ShapeDtypeStruct((dec_batch * steps, heads * d), F32),
                   jax.ShapeDtypeStruct(c_out.shape, c_out.dtype),
                   jax.ShapeDtypeStruct((dec_batch, heads, d), F32),
                   jax.ShapeDtypeStruct((dec_batch, heads, LANES), F32)],
        input_output_aliases={10: 1},
        compiler_params=_cparams("parallel"),
        name="mlstm_sample",
    )(qm, km, vm, om, ig, lf, g_m, c_all, n0, m0, c_out)


LOG2E = 1.4426950408889634


def _suffix_matrix(sc):
    ncol = -(-sc // LANES) * LANES
    j = np.arange(sc)[:, None]
    s = np.arange(ncol)[None, :]
    return jnp.asarray((j > s) & (s < sc), dtype=BF16)


def _sb_tile(z2, mask, carried, trix, sc):
    sp = jnp.log2(1.0 + jnp.exp2(-jnp.abs(z2)))
    ls = jnp.minimum(z2, 0.0) - sp
    l1 = -jnp.maximum(z2, 0.0) - sp
    if mask is not None:
        l1 = jnp.where(mask, l1, 0.0)
    res = _dot(l1.astype(BF16), trix)
    a = jnp.exp2(ls + res[:, :sc] + carried)
    if mask is not None:
        a = jnp.where(mask, a, 0.0)
    return a, carried + jnp.sum(l1, axis=-1, keepdims=True)


def _sb_prompt_kernel(qi_tab, kj_tab, q_ref, k_ref, v_ref, bias_ref, g_ref, trix_ref, buf_in, o_ref,
                      acc_ref, car_ref, *, tq, tk, sc, rc, scale):
    del buf_in
    p = pl.program_id(2)
    qi = qi_tab[p]
    kj = kj_tab[p]

    @pl.when(kj == qi)
    def _():
        acc_ref[...] = jnp.zeros_like(acc_ref)
        car_ref[...] = jnp.zeros_like(car_ref)

    def tile(masked):
        bias = bias_ref[...][:, :1] * LOG2E
        trix = trix_ref[...]

        def row_chunk(r, carry):
            rows = pl.ds(pl.multiple_of(r * rc, rc), rc)
            q = q_ref[rows, :]
            acc = acc_ref[rows, :]
            car = car_ref[rows, :1]
            if masked:
                row = r * rc + lax.broadcasted_iota(jnp.int32, (rc, sc), 0)
                col0 = lax.broadcasted_iota(jnp.int32, (rc, sc), 1)
            for sub in reversed(range(tk // sc)):
                ks = k_ref[sub * sc:(sub + 1) * sc, :]
                vs = v_ref[sub * sc:(sub + 1) * sc, :]
                mask = (col0 + sub * sc) < row if masked else None
                z2 = _dot_nt(q, ks) * (scale * LOG2E) + bias
                a, car = _sb_tile(z2, mask, car, trix, sc)
                acc = acc + _dot(a.astype(BF16), vs)
            acc_ref[rows, :] = acc
            car_ref[rows, :] = jnp.broadcast_to(car, (rc, LANES))
            return carry

        lax.fori_loop(0, tq // rc, row_chunk, 0)

    @pl.when(kj == qi)
    def _():
        tile(True)

    @pl.when(kj != qi)
    def _():
        tile(False)

    @pl.when(kj == 0)
    def _():
        o_ref[...] = _head_norm(acc_ref[...], g_ref[...]).astype(o_ref.dtype)


def sb_prompt(qs, ksb, vsb, bias, g_s, buf, *, batch, seq, heads, tq, sc, rc):
    tk = tq
    nq = seq // tq
    d = HEAD_DIM
    pairs = [(i, j) for i in range(nq) for j in range(i, -1, -1)]
    qi_tab = jnp.asarray([p[0] for p in pairs], jnp.int32)
    kj_tab = jnp.asarray([p[1] for p in pairs], jnp.int32)
    trix = _suffix_matrix(sc)
    grid_spec = pltpu.PrefetchScalarGridSpec(
        num_scalar_prefetch=2,
        grid=(batch, heads, len(pairs)),
        in_specs=[pl.BlockSpec((tq, d), lambda b, h, p, qt, kt: (b * nq + qt[p], h)),
                  pl.BlockSpec((tk, d), lambda b, h, p, qt, kt: (b * nq + kt[p], h)),
                  pl.BlockSpec((tk, d), lambda b, h, p, qt, kt: (b * nq + kt[p], h)),
                  pl.BlockSpec((None, 1, LANES), lambda b, h, p, qt, kt: (h, 0, 0)),
                  pl.BlockSpec((1, d), lambda b, h, p, qt, kt: (0, h)),
                  pl.BlockSpec(trix.shape, lambda b, h, p, qt, kt: (0, 0)),
                  pl.BlockSpec(memory_space=pl.ANY)],
        out_specs=pl.BlockSpec((tq, d), lambda b, h, p, qt, kt: (b * nq + qt[p], h)),
        scratch_shapes=[pltpu.VMEM((tq, d), F32), pltpu.VMEM((tq, LANES), F32)],
    )
    return pl.pallas_call(
        functools.partial(_sb_prompt_kernel, tq=tq, tk=tk, sc=sc, rc=rc, scale=HEAD_DIM ** -0.5),
        grid_spec=grid_spec,
        out_shape=jax.ShapeDtypeStruct(buf.shape, buf.dtype),
        input_output_aliases={8: 0},
        compiler_params=_cparams("parallel", "parallel", "arbitrary"),
        name="sb_prompt",
    )(qi_tab, kj_tab, qs, ksb, vsb, bias, g_s, trix, buf)


def _sb_sample_kernel(pt_ref, q_ref, kn_ref, vn_ref, bias_ref, g_ref, trix_ref, trixn_ref, *rest,
                      pg, heads, nj, scale, page, newpad):
    kp_refs = rest[:pg]
    vp_refs = rest[pg:2 * pg]
    o_ref, acc_ref, car_ref = rest[2 * pg:]
    j = pl.program_id(1)
    rows = heads * BF16_ROWS
    bias = bias_ref[:, :1] * LOG2E
    q = q_ref[...]
    qh = [q[h * BF16_ROWS:(h + 1) * BF16_ROWS, :] for h in range(heads)]

    def attend(keys, vals, mask, car, trix, sc):
        z2 = (jnp.concatenate([_dot_nt(qh[h], keys[h]) for h in range(heads)], axis=0) * (scale * LOG2E)
              + bias)
        a, car = _sb_tile(z2, mask, car, trix, sc)
        ab = a.astype(BF16)
        out = jnp.concatenate([_dot(ab[h * BF16_ROWS:(h + 1) * BF16_ROWS, :], vals[h]) for h in range(heads)],
                              axis=0)
        return out, car

    @pl.when(j == 0)
    def _():
        kn = kn_ref[...]
        vn = vn_ref[...]
        keys = [kn[:, h * HEAD_DIM:(h + 1) * HEAD_DIM] for h in range(heads)]
        vals = [vn[:, h * HEAD_DIM:(h + 1) * HEAD_DIM] for h in range(heads)]
        t_of_row = lax.broadcasted_iota(jnp.int32, (rows, newpad), 0) % BF16_ROWS
        mask = lax.broadcasted_iota(jnp.int32, (rows, newpad), 1) < t_of_row
        out, car = attend(keys, vals, mask, jnp.zeros((rows, 1), F32), trixn_ref[...], newpad)
        acc_ref[...] = out
        car_ref[...] = jnp.broadcast_to(car, car_ref.shape)

    def head_rows(refs, h):
        parts = [refs[i][pl.ds(h, page, stride=heads), :] for i in reversed(range(pg))]
        return jnp.concatenate(parts, axis=0).astype(BF16)

    keys = [head_rows(kp_refs, h) for h in range(heads)]
    vals = [head_rows(vp_refs, h) for h in range(heads)]
    out, car = attend(keys, vals, None, car_ref[:, :1], trix_ref[...], pg * page)
    acc = acc_ref[...] + out
    acc_ref[...] = acc
    car_ref[...] = jnp.broadcast_to(car, car_ref.shape)

    @pl.when(j == nj - 1)
    def _():
        o_ref[...] = _head_norm(acc, g_ref[...])


def sb_sample(q_rows, k_new, v_new, bias_rows, g_rows, cache_k, cache_v, page_table, *, layer_offset, pg, heads):
    bd, rows, d = q_rows.shape
    newpad = k_new.shape[1]
    n_pages = page_table.shape[0] // bd
    page = cache_k.shape[1] // heads
    width = heads * d
    assert n_pages % pg == 0 and page == LANES and rows == heads * BF16_ROWS
    nj = n_pages // pg
    trix = _suffix_matrix(pg * page)
    trixn = _suffix_matrix(newpad)

    def page_map(i):
        def index(b, j, pt):
            return (pt[b * n_pages + (n_pages - 1 - (j * pg + i))], 0, 0)
        return index

    per_b = lambda b, j, pt: (b, 0, 0)
    const2 = lambda b, j, pt: (0, 0)
    in_specs = [pl.BlockSpec((None, rows, d), per_b),
                pl.BlockSpec((None, newpad, width), per_b),
                pl.BlockSpec((None, newpad, width), per_b),
                pl.BlockSpec((rows, LANES), const2),
                pl.BlockSpec((rows, d), const2),
                pl.BlockSpec(trix.shape, const2),
                pl.BlockSpec(trixn.shape, const2)]
    in_specs += [pl.BlockSpec((None, page * heads, d), page_map(i)) for i in range(pg)]
    in_specs += [pl.BlockSpec((None, page * heads, d), page_map(i)) for i in range(pg)]
    grid_spec = pltpu.PrefetchScalarGridSpec(
        num_scalar_prefetch=1,
        grid=(bd, nj),
        in_specs=in_specs,
        out_specs=pl.BlockSpec((None, rows, d), per_b),
        scratch_shapes=[pltpu.VMEM((rows, d), F32), pltpu.VMEM((rows, LANES), F32)],
    )
    return pl.pallas_call(
        functools.partial(_sb_sample_kernel, pg=pg, heads=heads, nj=nj,
                          scale=HEAD_DIM ** -0.5, page=page, newpad=newpad),
        grid_spec=grid_spec,
        out_shape=jax.ShapeDtypeStruct((bd, rows, d), F32),
        compiler_params=_cparams("parallel", "arbitrary"),
        name="sb_sample",
    )(page_table + layer_offset, q_rows, k_new, v_new, bias_rows, g_rows, trix, trixn,
      *([cache_k] * pg), *([cache_v] * pg))


def kernel(x_prompt, x_sample, state_mlstm_C, state_mlstm_n, state_mlstm_m, cache_sb_k, cache_sb_v, page_table,
           p_prompt, p_sample, w_ffn1_gu, w_ffn1_dn, w_in, b_if, b_sb, w_out, g_head_m, g_head_s, w_ffn2_gu,
           w_ffn2_dn, w_ple_gate, w_ple_proj, ln_g, ln_b):
    batch, seq, d_model = x_prompt.shape
    dec_batch, dec_seq, _ = x_sample.shape
    depth = w_in.shape[0]
    h_m = state_mlstm_C.shape[2]
    h_s = cache_sb_k.shape[3]
    n_phys, page = cache_sb_k.shape[1], cache_sb_k.shape[2]
    d = HEAD_DIM
    w_m, w_s = h_m * d, h_s * d
    assert w_m == w_s and dec_seq <= BF16_ROWS
    d_ff = w_ffn1_dn.shape[1]
    alpha = (2 * depth) ** 0.25
    mp = batch * seq
    ms = dec_batch * dec_seq
    m = mp + ms
    chunk = MLSTM_CHUNK

    tm_big = _row_tile(m, 1088)
    tm_ln = _row_tile(m, 544)
    tf = 512
    fp = -(-d_ff // tf) * tf
    tn_dn = min(d_model, 512)
    tg = _row_tile(m, 512, mult=chunk)
    tb = min(seq, 1024)
    tq = min(seq, 512)
    sc = 128
    pg = 4
    nb = 4
    newpad = BF16_ROWS

    g0 = 4 * w_m
    wdn1 = w_ffn1_dn.astype(BF16)
    wdn2 = w_ffn2_dn.astype(BF16)
    win = w_in.astype(BF16)
    w_sb = win[:, :, g0 + 2 * h_m:]
    zpad = jnp.zeros((depth, d_model, LANES - h_m), BF16)
    w_gate = jnp.concatenate([win[:, :, g0:g0 + h_m], zpad, win[:, :, g0 + h_m:g0 + 2 * h_m], zpad], axis=-1)
    wout = w_out.astype(BF16)
    wpg = w_ple_gate.astype(BF16)
    wpp = w_ple_proj.astype(BF16)
    tr_cast = _row_tile(d_model, 128)

    bpad = jnp.zeros((depth, LANES - h_m), F32)
    bias_gate = jnp.concatenate([b_if[:, :h_m], bpad, b_if[:, h_m:], bpad], axis=-1).reshape(depth, 1, 2 * LANES)
    ck = cache_sb_k.reshape(depth * n_phys, page * h_s, d)
    cv = cache_sb_v.reshape(depth * n_phys, page * h_s, d)
    pt_flat = page_table.reshape(-1).astype(jnp.int32)

    x = jnp.concatenate([x_prompt.reshape(mp, d_model), x_sample.reshape(ms, d_model)], axis=0)
    xb = x.astype(BF16)
    p_all = jnp.concatenate([p_prompt.reshape(depth, mp, -1), p_sample.reshape(depth, ms, -1)], axis=1)

    pk = jnp.zeros((depth, mp * h_s, d), F32)
    pv = jnp.zeros((depth, mp * h_s, d), F32)
    sk = jnp.zeros((depth, ms * h_s, d), F32)
    sv = jnp.zeros((depth, ms * h_s, d), F32)
    s_c = jnp.zeros(state_mlstm_C.shape, F32)
    small = {k: [] for k in ("pC", "pn", "pm", "sn", "sm")}

    for l in range(depth):
        lg = ln_g[l].reshape(4, 1, d_model)
        lb = ln_b[l].reshape(4, 1, d_model)

        wg, wu = cast_gu(w_ffn1_gu, layer=l, fp=fp, tr=tr_cast)
        hmid = ffn_up(xb, wg, wu, tm=tm_big, tf=tf)
        x, xb = ffn_down(hmid, wdn1, x, lg[0], lb[0], layer=l, alpha=alpha, scale=0.5, tm=tm_ln, tn=tn_dn)

        qm, km, vm, om = proj_m(xb, win, layer=l, width=w_m, tm=tm_big)
        qs, ksb, vsb, pk, pv, sk, sv = proj_s(xb, w_sb, pk, pv, sk, sv, layer=l, width=w_s, heads=h_s,
                                              mp=mp, ms=ms)
        ig, lf, bq, cmq, rq, rt = gates(xb, w_gate, bias_gate, layer=l, heads=h_m, chunk=chunk, tg=tg)
        rrow = rt[:h_m, :mp].reshape(h_m, mp // chunk, chunk)
        g_m = g_head_m[l].reshape(1, w_m)
        g_s = g_head_s[l].reshape(1, w_s)

        hm, pC, pn, pm = mlstm_prompt(qm, km, vm, om, bq, cmq, rq, rrow, g_m, jnp.zeros((m, w_m), BF16),
                                      batch=batch, seq=seq, heads=h_m, chunk=chunk, tb=tb)
        m0 = jnp.broadcast_to(state_mlstm_m[l][:, :, None], (dec_batch, h_m, LANES))
        hm_s, s_c, sn, sm = mlstm_sample(qm, km, vm, om, ig, lf, g_m, state_mlstm_C, state_mlstm_n[l], m0, s_c,
                                         layer=l, row0=mp, nb=nb, steps=dec_seq, heads=h_m)
        hm = lax.dynamic_update_slice(hm, hm_s.astype(BF16), (mp, 0))

        bias_h = jnp.broadcast_to(b_sb[l].astype(F32)[:, None, None], (h_s, 1, LANES))
        hs = sb_prompt(qs, ksb, vsb, bias_h, g_s, jnp.zeros((m, w_s), BF16),
                       batch=batch, seq=seq, heads=h_s, tq=tq, sc=sc, rc=tq)

        rows = h_s * BF16_ROWS
        q_rows = qs[mp:].reshape(dec_batch, dec_seq, h_s, d).transpose(0, 2, 1, 3)
        q_rows = jnp.pad(q_rows, ((0, 0), (0, 0), (0, BF16_ROWS - dec_seq), (0, 0))).reshape(dec_batch, rows, d)
        kn = jnp.pad(ksb[mp:].reshape(dec_batch, dec_seq, w_s), ((0, 0), (0, newpad - dec_seq), (0, 0)))
        vn = jnp.pad(vsb[mp:].reshape(dec_batch, dec_seq, w_s), ((0, 0), (0, newpad - dec_seq), (0, 0)))
        bias_rows = jnp.broadcast_to(jnp.repeat(b_sb[l].astype(F32), BF16_ROWS)[:, None], (rows, LANES))
        g_rows = jnp.repeat(g_head_s[l].reshape(h_s, d), BF16_ROWS, axis=0)
        hs_s = sb_sample(q_rows, kn, vn, bias_rows, g_rows, ck, cv, pt_flat, layer_offset=l * n_phys, pg=pg,
                         heads=h_s)
        hs_s = hs_s.reshape(dec_batch, h_s, BF16_ROWS, d)[:, :, :dec_seq].transpose(0, 2, 1, 3).reshape(ms, w_s)
        hs = lax.dynamic_update_slice(hs, hs_s.astype(BF16), (mp, 0))

        x, xb = out_proj(hm, hs, wout, x, lg[1], lb[1], layer=l, alpha=alpha, tm=tm_ln)

        wg, wu = cast_gu(w_ffn2_gu, layer=l, fp=fp, tr=tr_cast)
        hmid = ffn_up(xb, wg, wu, tm=tm_big, tf=tf)
        x, xb = ffn_down(hmid, wdn2, x, lg[2], lb[2], layer=l, alpha=alpha, scale=0.5, tm=tm_ln, tn=tn_dn)

        x, xb = ple(xb, wpg, p_all, wpp, x, lg[3], lb[3], layer=l, alpha=alpha, tm=tm_ln)

        small["pC"].append(pC)
        small["pn"].append(pn.reshape(batch, h_m, d))
        small["pm"].append(pm[:, :, 0, 0])
        small["sn"].append(sn)
        small["sm"].append(sm[:, :, 0])

    st = lambda k: jnp.stack(small[k], axis=0)
    return (x[:mp].reshape(batch, seq, d_model), x[mp:].reshape(dec_batch, dec_seq, d_model),
            st("pC"), st("pn"), st("pm"),
            pk.reshape(depth, batch, seq, h_s, d), pv.reshape(depth, batch, seq, h_s, d),
            s_c, st("sn"), st("sm"),
            sk.reshape(depth, dec_batch, dec_seq, h_s, d), sv.reshape(depth, dec_batch, dec_seq, h_s, d))
```
